```python
import jax, jax.numpy as jnp
from jax import lax
import numpy as np

D_MODEL = 1024
BATCH = 2
SEQ = 8192
DEPTH = 2
DEC_BATCH = 32
DEC_SEQ = 4
PAST_LEN = 8192
PAGE_SIZE = 128

N_A_LAYERS = DEPTH // 2
N_B_LAYERS = DEPTH - N_A_LAYERS
GDN_HEADS = 8
GDN_DK = 128
GDN_DV = 128
GDN_QK_DIM = GDN_HEADS * GDN_DK
GDN_V_DIM = GDN_HEADS * GDN_DV
GDN_CONV_DIM = 2 * GDN_QK_DIM + GDN_V_DIM
GDN_PROJ = GDN_CONV_DIM + GDN_V_DIM + 2 * GDN_HEADS
CONV_W = 4
GDN_CHUNK = 64
NSA_HEADS = 16
NSA_KV_HEADS = 4
NSA_HPG = NSA_HEADS // NSA_KV_HEADS
NSA_DH = D_MODEL // NSA_HEADS
CMP_BLOCK = 32
CMP_STRIDE = 16
CMP_HIDDEN = NSA_DH
SEL_BLOCK = 64
N_SEL = 16
WINDOW = 512
Q_BLOCK = 128
NSA_QG = NSA_HEADS * NSA_DH + 3 * NSA_HEADS
NSA_KV_PROJ = 6 * NSA_KV_HEADS * NSA_DH
PEER_HEADS = 8
PEER_DQ = 256
PEER_HALF = PEER_DQ // 2
N_KEYS = 128
N_EXPERTS = N_KEYS * N_KEYS
PEER_TOPK = 16
PEER_TOKEN_BLOCK = 128
EPS = 1e-6
NEG = -1e30
BIG = 1e30

kernel_name = 'yoco_gdn_nsa_peer_step'


def rmsnorm(x, gain):
    xf = x.astype(jnp.float32)
    y = xf * lax.rsqrt(jnp.mean(xf * xf, axis=-1, keepdims=True) + EPS)
    return (y * gain.astype(jnp.float32)).astype(x.dtype)


def l2norm(x):
    xf = x.astype(jnp.float32)
    return xf * lax.rsqrt(jnp.sum(xf * xf, axis=-1, keepdims=True) + EPS)


def masked_softmax(s, mask):
    s = jnp.where(mask, s, NEG)
    m = jnp.max(s, axis=-1, keepdims=True)
    p = jnp.exp(s - m) * mask
    return p / jnp.maximum(jnp.sum(p, axis=-1, keepdims=True), 1e-30)


def alibi_slopes():
    h = jnp.arange(1, NSA_HEADS + 1, dtype=jnp.float32)
    return jnp.exp2(-8.0 * h / NSA_HEADS).reshape(NSA_KV_HEADS, NSA_HPG)


def short_conv(u, buf0, w):
    L = u.shape[1]
    buf = jnp.concatenate([buf0.astype(u.dtype), u], axis=1)
    y = sum(buf[:, i:i + L] * w[i] for i in range(CONV_W))
    return jax.nn.silu(y), buf[:, L:]


def gated_delta_rule(q, k, v, g, beta, s0):
    Bsz, L, H, DK = q.shape
    C = GDN_CHUNK if L % GDN_CHUNK == 0 else L
    n = L // C

    def chunks(t):
        t = t.reshape((Bsz, n, C, H) + t.shape[3:])
        return jnp.moveaxis(t, (1, 3), (0, 2))

    qc, kc, vc, gc, bc = chunks(q), chunks(k), chunks(v), chunks(g), chunks(beta)
    gcum = jnp.cumsum(gc, axis=-1)
    idx = jnp.arange(C)
    lower = idx[:, None] >= idx[None, :]
    strict = idx[:, None] > idx[None, :]
    diff = gcum[..., :, None] - gcum[..., None, :]
    decay = jnp.where(lower, jnp.exp(jnp.where(lower, diff, 0.0)), 0.0)
    kb = kc * bc[..., None]
    a_mat = jnp.where(strict, jnp.einsum('nbhik,nbhjk->nbhij', kb, kc) * decay, 0.0)
    eye = jnp.eye(C, dtype=jnp.float32)
    t_mat = lax.linalg.triangular_solve(eye + a_mat, jnp.broadcast_to(eye, a_mat.shape),
                                        left_side=True, lower=True, unit_diagonal=True)
    u = t_mat @ (vc * bc[..., None])
    w = t_mat @ (kb * jnp.exp(gcum)[..., None])
    qk = jnp.where(lower, jnp.einsum('nbhik,nbhjk->nbhij', qc, kc) * decay, 0.0)

    def step(S, xs):
        q_i, k_i, u_i, w_i, g_i, qk_i = xs
        v_new = u_i - w_i @ S
        o = (q_i * jnp.exp(g_i)[..., None]) @ S + qk_i @ v_new
        g_last = g_i[..., -1]
        k_dec = k_i * jnp.exp(g_last[..., None] - g_i)[..., None]
        S = S * jnp.exp(g_last)[..., None, None] + jnp.einsum('bhck,bhcv->bhkv', k_dec, v_new)
        return S, o

    S, o = lax.scan(step, s0, (qc, kc, u, w, gcum, qk))
    o = jnp.moveaxis(o, (0, 2), (1, 3)).reshape(Bsz, L, H, v.shape[-1])
    return o, S


def gdn_mixer(h, s0, conv0, w_in, conv_w, a_log, dt_bias, o_gain, w_out):
    B, L, _ = h.shape
    proj = h @ w_in
    qkv, conv_state = short_conv(proj[..., :GDN_CONV_DIM], conv0, conv_w)
    z = proj[..., GDN_CONV_DIM:GDN_CONV_DIM + GDN_V_DIM].reshape(B, L, GDN_HEADS, GDN_DV)
    a = proj[..., GDN_CONV_DIM + GDN_V_DIM:GDN_CONV_DIM + GDN_V_DIM + GDN_HEADS].astype(jnp.float32)
    b = proj[..., GDN_CONV_DIM + GDN_V_DIM + GDN_HEADS:].astype(jnp.float32)
    q = l2norm(qkv[..., :GDN_QK_DIM].reshape(B, L, GDN_HEADS, GDN_DK)) * (GDN_DK ** -0.5)
    k = l2norm(qkv[..., GDN_QK_DIM:2 * GDN_QK_DIM].reshape(B, L, GDN_HEADS, GDN_DK))
    v = qkv[..., 2 * GDN_QK_DIM:].reshape(B, L, GDN_HEADS, GDN_DV).astype(jnp.float32)
    g = -jnp.exp(a_log.astype(jnp.float32)) * jax.nn.softplus(a + dt_bias.astype(jnp.float32))
    beta = jax.nn.sigmoid(b)
    o, s_new = gated_delta_rule(q, k, v, g, beta, s0.astype(jnp.float32))
    o = rmsnorm(o, o_gain) * jax.nn.silu(z.astype(jnp.float32))
    return o.reshape(B, L, GDN_V_DIM).astype(h.dtype) @ w_out, s_new, conv_state


def peer(h, w_q, sub_keys, u, v):
    shape = h.shape
    flat = h.reshape(-1, D_MODEL)
    n = flat.shape[0]
    blk = PEER_TOKEN_BLOCK
    nb = -(-n // blk)
    flat = jnp.pad(flat, ((0, nb * blk - n), (0, 0))).reshape(nb, blk, D_MODEL)

    def block(hb):
        q = (hb @ w_q).reshape(blk, PEER_HEADS, 2, PEER_HALF)
        s = jnp.einsum('thpd,hpkd->thpk', q, sub_keys, preferred_element_type=jnp.float32)
        s1, i1 = lax.top_k(s[:, :, 0], PEER_TOPK)
        s2, i2 = lax.top_k(s[:, :, 1], PEER_TOPK)
        cand = (s1[..., :, None] + s2[..., None, :]).reshape(blk, PEER_HEADS, PEER_TOPK * PEER_TOPK)
        cidx = (i1[..., :, None] * N_KEYS + i2[..., None, :]).reshape(blk, PEER_HEADS, PEER_TOPK * PEER_TOPK)
        top_s, pos = lax.top_k(cand, PEER_TOPK)
        eidx = jnp.take_along_axis(cidx, pos, axis=-1)
        gate = jax.nn.softmax(top_s, axis=-1)
        ue = u[eidx]
        ve = v[eidx]
        act = jax.nn.gelu(jnp.einsum('td,thkd->thk', hb, ue, preferred_element_type=jnp.float32))
        return jnp.einsum('thk,thkd->td', (gate * act).astype(ve.dtype), ve,
                          preferred_element_type=jnp.float32)

    out = lax.map(block, flat).reshape(nb * blk, D_MODEL)[:n]
    return out.reshape(shape).astype(h.dtype)


def shared_kv_rows(x, kv_ln, kv_w, k_gain):
    B, L, _ = x.shape
    kv = (rmsnorm(x, kv_ln) @ kv_w).reshape(B, L, 6, NSA_KV_HEADS, NSA_DH)
    k_sel = rmsnorm(kv[:, :, 2], k_gain[1])
    k_win = rmsnorm(kv[:, :, 4], k_gain[2])
    rows = jnp.stack([kv[:, :, 0], kv[:, :, 1], k_sel, kv[:, :, 3]], axis=2)
    win = jnp.stack([k_win, kv[:, :, 5]], axis=2)
    return rows, win


def compress(seq, w1, pe, w2):
    B, T, G, Dh = seq.shape
    n_chunk = T // CMP_STRIDE
    c = seq[:, :n_chunk * CMP_STRIDE].reshape(B, n_chunk, CMP_STRIDE, G, Dh)
    first = jnp.einsum('bcrgd,rdh->bcgh', c, w1[:CMP_STRIDE])
    second = jnp.einsum('bcrgd,rdh->bcgh', c, w1[CMP_STRIDE:])
    pe_term = jnp.einsum('rd,rdh->h', pe, w1)
    hid = jax.nn.gelu(first[:, :-1] + second[:, 1:] + pe_term)
    return jnp.einsum('bcgh,hd->bcgd', hid, w2)


def nsa_context(rows, win_all, win_pos0, q_block, band_len, k_gain, phi_w1, phi_pe, phi_w2):
    B, T = rows.shape[:2]
    k_cmp = rmsnorm(compress(rows[:, :, 0], phi_w1[0], phi_pe[0], phi_w2[0]), k_gain[0])
    v_cmp = compress(rows[:, :, 1], phi_w1[1], phi_pe[1], phi_w2[1])
    n_cmp = k_cmp.shape[1]
    cmp_start = jnp.arange(n_cmp, dtype=jnp.int32) * CMP_STRIDE
    cmp_end = cmp_start + CMP_BLOCK - 1
    n_sel = -(-T // SEL_BLOCK)
    sel = jnp.pad(rows[:, :, 2:], ((0, 0), (0, n_sel * SEL_BLOCK - T), (0, 0), (0, 0), (0, 0)))
    sel = sel.reshape(B, n_sel, SEL_BLOCK, 2, NSA_KV_HEADS, NSA_DH).transpose(3, 0, 4, 1, 2, 5)
    sel_start = jnp.arange(n_sel, dtype=jnp.int32) * SEL_BLOCK
    overlap = ((cmp_start[:, None] < sel_start[None, :] + SEL_BLOCK)
               & (cmp_end[:, None] >= sel_start[None, :])).astype(jnp.float32)
    return (k_cmp, v_cmp, cmp_end, sel[0], sel[1], overlap,
            win_all[:, :, 0], win_all[:, :, 1], win_pos0, q_block, band_len)


def nsa_mixer(h, pos0, ctx, w_qg, q_gain, w_out):
    k_cmp, v_cmp, cmp_end, k_sel, v_sel, overlap, k_win, v_win, win_pos0, q_block, band_len = ctx
    B, L, _ = h.shape
    G, HPG, DH = NSA_KV_HEADS, NSA_HPG, NSA_DH
    nb = L // q_block
    proj = h @ w_qg
    q = rmsnorm(proj[..., :NSA_HEADS * DH].reshape(B, L, G, HPG, DH), q_gain)
    gates = jax.nn.sigmoid(proj[..., NSA_HEADS * DH:].astype(jnp.float32)).reshape(B, L, 3, G, HPG)
    q_b = q.reshape(B, nb, q_block, G, HPG, DH).swapaxes(0, 1)
    g_b = gates.reshape(B, nb, q_block, 3, G, HPG).swapaxes(0, 1)
    t_b = (pos0 + jnp.arange(L, dtype=jnp.int32)).reshape(nb, q_block)
    starts = jnp.arange(nb, dtype=jnp.int32) * q_block
    slopes = alibi_slopes()[None, None, :, :, None]
    n_sel = k_sel.shape[2]
    k_top = min(N_SEL, n_sel)
    blk_ids = jnp.arange(n_sel, dtype=jnp.int32)
    bi = jnp.arange(B)[:, None, None, None]
    gi = jnp.arange(G)[None, None, :, None]
    scale = NSA_DH ** -0.5

    def block(args):
        qq, gg, tt, st = args
        dist_c = (tt[:, None] - cmp_end[None, :]).astype(jnp.float32)
        s_c = jnp.einsum('bqgnd,bcgd->bqgnc', qq, k_cmp, preferred_element_type=jnp.float32) * scale
        s_c = s_c - slopes * dist_c[None, :, None, None, :]
        p_c = masked_softmax(s_c, (dist_c >= 0)[None, :, None, None, :])
        o_c = jnp.einsum('bqgnc,bcgd->bqgnd', p_c.astype(v_cmp.dtype), v_cmp,
                         preferred_element_type=jnp.float32)
        imp = jnp.einsum('bqgnc,cj->bqgj', p_c, overlap)
        cur = tt // SEL_BLOCK
        forced = ((blk_ids[None] == 0) | (blk_ids[None] == cur[:, None])
                  | (blk_ids[None] == cur[:, None] - 1))
        visible = blk_ids[None] * SEL_BLOCK <= tt[:, None]
        score = jnp.where(forced[None, :, None], BIG, jnp.where(visible[None, :, None], imp, NEG))
        _, idx = lax.top_k(score, k_top)
        qb = tt.shape[0]
        ks = k_sel[bi, gi, idx].reshape(B, qb, G, k_top * SEL_BLOCK, DH)
        vs = v_sel[bi, gi, idx].reshape(B, qb, G, k_top * SEL_BLOCK, DH)
        spos = (idx[..., None] * SEL_BLOCK + jnp.arange(SEL_BLOCK, dtype=jnp.int32)).reshape(
            B, qb, G, k_top * SEL_BLOCK)
        dist_s = (tt[None, :, None, None] - spos).astype(jnp.float32)
        s_s = jnp.einsum('bqgnd,bqgkd->bqgnk', qq, ks, preferred_element_type=jnp.float32) * scale
        s_s = s_s - slopes * dist_s[:, :, :, None]
        p_s = masked_softmax(s_s, (dist_s >= 0)[:, :, :, None])
        o_s = jnp.einsum('bqgnk,bqgkd->bqgnd', p_s.astype(vs.dtype), vs,
                         preferred_element_type=jnp.float32)
        kw = lax.dynamic_slice_in_dim(k_win, st, band_len, axis=1)
        vw = lax.dynamic_slice_in_dim(v_win, st, band_len, axis=1)
        wpos = win_pos0 + st + jnp.arange(band_len, dtype=jnp.int32)
        dist_w = tt[:, None] - wpos[None, :]
        mask_w = (wpos[None, :] >= 0) & (dist_w >= 0) & (dist_w < WINDOW)
        s_w = jnp.einsum('bqgnd,blgd->bqgnl', qq, kw, preferred_element_type=jnp.float32) * scale
        s_w = s_w - slopes * dist_w.astype(jnp.float32)[None, :, None, None, :]
        p_w = masked_softmax(s_w, mask_w[None, :, None, None, :])
        o_w = jnp.einsum('bqgnl,blgd->bqgnd', p_w.astype(vw.dtype), vw,
                         preferred_element_type=jnp.float32)
        return (gg[:, :, 0, ..., None] * o_c + gg[:, :, 1, ..., None] * o_s
                + gg[:, :, 2, ..., None] * o_w)

    o = lax.map(block, (q_b, g_b, t_b, starts))
    o = o.swapaxes(0, 1).reshape(B, L, NSA_HEADS * DH).astype(h.dtype)
    return o @ w_out


def setup_inputs(seed: int = 0) -> dict:
    key = jax.random.key(seed)
    ks = jax.random.split(key, 32)
    f32 = jnp.float32

    def nrm(k, shape, scale):
        return jax.random.normal(k, shape, f32) * scale

    def gain(k, shape):
        return 1.0 + 0.05 * jax.random.normal(k, shape, f32)

    n_pages = PAST_LEN // PAGE_SIZE
    n_used = DEC_BATCH * n_pages
    n_pool = n_used + max(1, n_used // 4)
    win_buf = min(WINDOW, PAST_LEN)
    perm = jax.random.permutation(ks[6], n_pool)
    page_table = perm[:n_used].reshape(DEC_BATCH, n_pages).astype(jnp.int32)
    return {
        'x_prompt': nrm(ks[0], (BATCH, SEQ, D_MODEL), 1.0),
        'x_sample': nrm(ks[1], (DEC_BATCH, DEC_SEQ, D_MODEL), 1.0),
        'cache_kv': nrm(ks[2], (n_pool, PAGE_SIZE, 4, NSA_KV_HEADS, NSA_DH), 1.0),
        'cache_win': nrm(ks[3], (DEC_BATCH, win_buf, 2, NSA_KV_HEADS, NSA_DH), 1.0),
        'state_gdn': nrm(ks[4], (N_A_LAYERS, DEC_BATCH, GDN_HEADS, GDN_DK, GDN_DV), 0.1),
        'state_conv': nrm(ks[5], (N_A_LAYERS, DEC_BATCH, CONV_W - 1, GDN_CONV_DIM), 1.0),
        'page_table': page_table,
        'a_ln': gain(ks[7], (N_A_LAYERS, D_MODEL)),
        'a_w_in': nrm(ks[8], (N_A_LAYERS, D_MODEL, GDN_PROJ), D_MODEL ** -0.5),
        'a_conv_w': nrm(ks[9], (N_A_LAYERS, CONV_W, GDN_CONV_DIM), 0.5),
        'a_a_log': jnp.log(jax.random.uniform(ks[10], (N_A_LAYERS, GDN_HEADS), f32, 1.0, 16.0)),
        'a_dt_bias': jax.random.uniform(ks[11], (N_A_LAYERS, GDN_HEADS), f32, -4.0, -2.0),
        'a_o_gain': gain(ks[12], (N_A_LAYERS, GDN_DV)),
        'a_w_out': nrm(ks[13], (N_A_LAYERS, GDN_V_DIM, D_MODEL), GDN_V_DIM ** -0.5),
        'kv_ln': gain(ks[14], (D_MODEL,)),
        'kv_w': nrm(ks[15], (D_MODEL, NSA_KV_PROJ), D_MODEL ** -0.5),
        'kv_k_gain': gain(ks[16], (3, NSA_DH)),
        'phi_w1': nrm(ks[17], (2, CMP_BLOCK, NSA_DH, CMP_HIDDEN), (CMP_BLOCK * NSA_DH) ** -0.5),
        'phi_pe': nrm(ks[18], (2, CMP_BLOCK, NSA_DH), 0.1),
        'phi_w2': nrm(ks[19], (2, CMP_HIDDEN, NSA_DH), CMP_HIDDEN ** -0.5),
        'b_ln': gain(ks[20], (N_B_LAYERS, D_MODEL)),
        'b_w_qg': nrm(ks[21], (N_B_LAYERS, D_MODEL, NSA_QG), D_MODEL ** -0.5),
        'b_q_gain': gain(ks[22], (N_B_LAYERS, NSA_DH)),
        'b_w_out': nrm(ks[23], (N_B_LAYERS, NSA_HEADS * NSA_DH, D_MODEL), (NSA_HEADS * NSA_DH) ** -0.5),
        'p_ln': gain(ks[24], (DEPTH, D_MODEL)),
        'p_w_q': nrm(ks[25], (DEPTH, D_MODEL, PEER_HEADS * PEER_DQ), D_MODEL ** -0.5),
        'p_sub_keys': nrm(ks[26], (DEPTH, PEER_HEADS, 2, N_KEYS, PEER_HALF), PEER_HALF ** -0.5),
        'p_u': nrm(ks[27], (DEPTH, N_EXPERTS, D_MODEL), D_MODEL ** -0.5),
        'p_v': nrm(ks[28], (DEPTH, N_EXPERTS, D_MODEL), (PEER_HEADS * PEER_TOPK) ** -0.5),
    }


def reference(x_prompt, x_sample, cache_kv, cache_win, state_gdn, state_conv, page_table,
              a_ln, a_w_in, a_conv_w, a_a_log, a_dt_bias, a_o_gain, a_w_out,
              kv_ln, kv_w, kv_k_gain, phi_w1, phi_pe, phi_w2,
              b_ln, b_w_qg, b_q_gain, b_w_out,
              p_ln, p_w_q, p_sub_keys, p_u, p_v):
    xp, xs = x_prompt, x_sample
    bp, lp = xp.shape[0], xp.shape[1]
    bs, ls = xs.shape[0], xs.shape[1]
    past_len = page_table.shape[1] * cache_kv.shape[1]
    win_buf = cache_win.shape[1]
    q_block_p = Q_BLOCK if lp % Q_BLOCK == 0 else lp
    s_p, c_p, s_s, c_s = [], [], [], []
    for layer in range(DEPTH):
        if layer < N_A_LAYERS:
            i = layer
            zs = jnp.zeros((bp, GDN_HEADS, GDN_DK, GDN_DV), jnp.float32)
            zc = jnp.zeros((bp, CONV_W - 1, GDN_CONV_DIM), xp.dtype)
            yp, sp, cp = gdn_mixer(rmsnorm(xp, a_ln[i]), zs, zc, a_w_in[i], a_conv_w[i],
                                   a_a_log[i], a_dt_bias[i], a_o_gain[i], a_w_out[i])
            ys, ss, cs = gdn_mixer(rmsnorm(xs, a_ln[i]), state_gdn[i], state_conv[i], a_w_in[i],
                                   a_conv_w[i], a_a_log[i], a_dt_bias[i], a_o_gain[i], a_w_out[i])
            xp = xp + yp
            xs = xs + ys
            s_p.append(sp)
            c_p.append(cp)
            s_s.append(ss)
            c_s.append(cs)
        else:
            j = layer - N_A_LAYERS
            xp = xp + nsa_mixer(rmsnorm(xp, b_ln[j]), 0, ctx_p, b_w_qg[j], b_q_gain[j], b_w_out[j])
            xs = xs + nsa_mixer(rmsnorm(xs, b_ln[j]), past_len, ctx_s, b_w_qg[j], b_q_gain[j], b_w_out[j])
        xp = xp + peer(rmsnorm(xp, p_ln[layer]), p_w_q[layer], p_sub_keys[layer], p_u[layer], p_v[layer])
        xs = xs + peer(rmsnorm(xs, p_ln[layer]), p_w_q[layer], p_sub_keys[layer], p_u[layer], p_v[layer])
        if layer == N_A_LAYERS - 1:
            rows_p, win_p = shared_kv_rows(xp, kv_ln, kv_w, kv_k_gain)
            rows_s, win_s = shared_kv_rows(xs, kv_ln, kv_w, kv_k_gain)
            win_pad_p = jnp.pad(win_p, ((0, 0), (WINDOW, 0), (0, 0), (0, 0), (0, 0)))
            ctx_p = nsa_context(rows_p, win_pad_p, -WINDOW, q_block_p, WINDOW + q_block_p,
                                kv_k_gain, phi_w1, phi_pe, phi_w2)
            past = cache_kv[page_table].reshape((bs, past_len) + cache_kv.shape[2:])
            full_s = jnp.concatenate([past.astype(rows_s.dtype), rows_s], axis=1)
            win_all_s = jnp.concatenate([cache_win.astype(win_s.dtype), win_s], axis=1)
            ctx_s = nsa_context(full_s, win_all_s, past_len - win_buf, ls, win_buf + ls,
                                kv_k_gain, phi_w1, phi_pe, phi_w2)
            win_prompt = win_p[:, lp - min(WINDOW, lp):]
            win_sample = win_all_s[:, ls:]
    return (xp, xs, rows_p, win_prompt, jnp.stack(s_p), jnp.stack(c_p),
            rows_s, win_sample, jnp.stack(s_s), jnp.stack(c_s))
```

```python
import functools

import jax
import jax.numpy as jnp
from jax import lax
from jax.experimental import pallas as pl
from jax.experimental.pallas import tpu as pltpu

D_MODEL = 1024
GDN_HEADS = 8
GDN_DK = 128
GDN_DV = 128
GDN_QK_DIM = GDN_HEADS * GDN_DK
GDN_V_DIM = GDN_HEADS * GDN_DV
GDN_CONV_DIM = 2 * GDN_QK_DIM + GDN_V_DIM
CONV_W = 4
GDN_CHUNK = 64
NSA_HEADS = 16
NSA_KV_HEADS = 4
NSA_HPG = NSA_HEADS // NSA_KV_HEADS
NSA_DH = D_MODEL // NSA_HEADS
CMP_BLOCK = 32
CMP_STRIDE = 16
SEL_BLOCK = 64
N_SEL = 16
WINDOW = 512
Q_BLOCK = 128
PEER_HEADS = 8
PEER_DQ = 256
PEER_HALF = PEER_DQ // 2
N_KEYS = 128
PEER_TOPK = 16
PEER_TOKEN_BLOCK = 128
EPS = 1e-6
NEG = -1e30
BIG = 1e30

LANE = 128


def _norm_matmul_body(x_ref, g_ref, w_ref, o_ref, *, normalize):
    x = x_ref[...]
    if normalize:
        x = x * lax.rsqrt(jnp.mean(x * x, axis=-1, keepdims=True) + EPS) * g_ref[...]
    o_ref[...] = jnp.dot(x.astype(jnp.bfloat16), w_ref[...], preferred_element_type=jnp.float32)


def _norm_matmul(x, gain, w, *, normalize=True):
    m, k = x.shape
    n = w.shape[1]
    n_pad = -(-n // LANE) * LANE
    w = jnp.pad(w, ((0, 0), (0, n_pad - n))).astype(jnp.bfloat16)
    tm = 512 if m % 512 == 0 else m
    tn = next(t for t in (512, 384, 256, 128) if n_pad % t == 0)
    if gain is None:
        gain = jnp.ones((k,), jnp.float32)
    out = pl.pallas_call(
        functools.partial(_norm_matmul_body, normalize=normalize),
        grid=(m // tm, n_pad // tn),
        in_specs=[pl.BlockSpec((tm, k), lambda i, j: (i, 0)),
                  pl.BlockSpec((1, k), lambda i, j: (0, 0)),
                  pl.BlockSpec((k, tn), lambda i, j: (0, j))],
        out_specs=pl.BlockSpec((tm, tn), lambda i, j: (i, j)),
        out_shape=jax.ShapeDtypeStruct((m, n_pad), jnp.float32),
        compiler_params=pltpu.CompilerParams(dimension_semantics=("arbitrary", "arbitrary")),
        name="norm_matmul",
    )(x, gain.reshape(1, k).astype(jnp.float32), w)
    return out[:, :n]


def _proj(x, gain, w, normalize=True):
    b, l, d = x.shape
    return _norm_matmul(x.reshape(b * l, d), gain, w, normalize=normalize).reshape(b, l, -1)


def _rmsnorm(x, gain):
    xf = x.astype(jnp.float32)
    y = xf * lax.rsqrt(jnp.mean(xf * xf, axis=-1, keepdims=True) + EPS)
    return (y * gain.astype(jnp.float32)).astype(x.dtype)


def _l2norm(x):
    xf = x.astype(jnp.float32)
    return xf * lax.rsqrt(jnp.sum(xf * xf, axis=-1, keepdims=True) + EPS)


def _masked_softmax(s, mask):
    s = jnp.where(mask, s, NEG)
    m = jnp.max(s, axis=-1, keepdims=True)
    p = jnp.exp(s - m) * mask
    return p / jnp.maximum(jnp.sum(p, axis=-1, keepdims=True), 1e-30)


def _alibi_slopes():
    h = jnp.arange(1, NSA_HEADS + 1, dtype=jnp.float32)
    return jnp.exp2(-8.0 * h / NSA_HEADS).reshape(NSA_KV_HEADS, NSA_HPG)


def _short_conv(u, buf0, w):
    L = u.shape[1]
    buf = jnp.concatenate([buf0.astype(u.dtype), u], axis=1)
    y = sum(buf[:, i:i + L] * w[i] for i in range(CONV_W))
    return jax.nn.silu(y), buf[:, L:]


def _gated_delta_rule(q, k, v, g, beta, s0):
    Bsz, L, H, DK = q.shape
    C = GDN_CHUNK if L % GDN_CHUNK == 0 else L
    n = L // C

    def chunks(t):
        t = t.reshape((Bsz, n, C, H) + t.shape[3:])
        return jnp.moveaxis(t, (1, 3), (0, 2))

    qc, kc, vc, gc, bc = chunks(q), chunks(k), chunks(v), chunks(g), chunks(beta)
    gcum = jnp.cumsum(gc, axis=-1)
    idx = jnp.arange(C)
    lower = idx[:, None] >= idx[None, :]
    strict = idx[:, None] > idx[None, :]
    diff = gcum[..., :, None] - gcum[..., None, :]
    decay = jnp.where(lower, jnp.exp(jnp.where(lower, diff, 0.0)), 0.0)
    kb = kc * bc[..., None]
    a_mat = jnp.where(strict, jnp.einsum('nbhik,nbhjk->nbhij', kb, kc) * decay, 0.0)
    eye = jnp.eye(C, dtype=jnp.float32)
    t_mat = lax.linalg.triangular_solve(eye + a_mat, jnp.broadcast_to(eye, a_mat.shape),
                                        left_side=True, lower=True, unit_diagonal=True)
    u = t_mat @ (vc * bc[..., None])
    w = t_mat @ (kb * jnp.exp(gcum)[..., None])
    qk = jnp.where(lower, jnp.einsum('nbhik,nbhjk->nbhij', qc, kc) * decay, 0.0)

    def step(S, xs):
        q_i, k_i, u_i, w_i, g_i, qk_i = xs
        v_new = u_i - w_i @ S
        o = (q_i * jnp.exp(g_i)[..., None]) @ S + qk_i @ v_new
        g_last = g_i[..., -1]
        k_dec = k_i * jnp.exp(g_last[..., None] - g_i)[..., None]
        S = S * jnp.exp(g_last)[..., None, None] + jnp.einsum('bhck,bhcv->bhkv', k_dec, v_new)
        return S, o

    S, o = lax.scan(step, s0, (qc, kc, u, w, gcum, qk))
    o = jnp.moveaxis(o, (0, 2), (1, 3)).reshape(Bsz, L, H, v.shape[-1])
    return o, S


def _gdn_mixer(x, ln, s0, conv0, w_in, conv_w, a_log, dt_bias, o_gain, w_out):
    B, L, _ = x.shape
    proj = _proj(x, ln, w_in)
    qkv, conv_state = _short_conv(proj[..., :GDN_CONV_DIM], conv0, conv_w)
    z = proj[..., GDN_CONV_DIM:GDN_CONV_DIM + GDN_V_DIM].reshape(B, L, GDN_HEADS, GDN_DV)
    a = proj[..., GDN_CONV_DIM + GDN_V_DIM:GDN_CONV_DIM + GDN_V_DIM + GDN_HEADS]
    b = proj[..., GDN_CONV_DIM + GDN_V_DIM + GDN_HEADS:]
    q = _l2norm(qkv[..., :GDN_QK_DIM].reshape(B, L, GDN_HEADS, GDN_DK)) * (GDN_DK ** -0.5)
    k = _l2norm(qkv[..., GDN_QK_DIM:2 * GDN_QK_DIM].reshape(B, L, GDN_HEADS, GDN_DK))
    v = qkv[..., 2 * GDN_QK_DIM:].reshape(B, L, GDN_HEADS, GDN_DV)
    g = -jnp.exp(a_log) * jax.nn.softplus(a + dt_bias)
    beta = jax.nn.sigmoid(b)
    o, s_new = _gated_delta_rule(q, k, v, g, beta, s0)
    o = _rmsnorm(o, o_gain) * jax.nn.silu(z)
    return _proj(o.reshape(B, L, GDN_V_DIM), None, w_out, normalize=False), s_new, conv_state


def _peer(x, ln, w_q, sub_keys, u, v):
    shape = x.shape
    h = _rmsnorm(x, ln).reshape(-1, D_MODEL)
    qall = _norm_matmul(x.reshape(-1, D_MODEL), ln, w_q)
    n = h.shape[0]
    blk = PEER_TOKEN_BLOCK
    nb = n // blk
    h = h.reshape(nb, blk, D_MODEL)
    qall = qall.reshape(nb, blk, PEER_HEADS, 2, PEER_HALF)

    def block(args):
        hb, q = args
        s = jnp.einsum('thpd,hpkd->thpk', q, sub_keys, preferred_element_type=jnp.float32)
        s1, i1 = lax.top_k(s[:, :, 0], PEER_TOPK)
        s2, i2 = lax.top_k(s[:, :, 1], PEER_TOPK)
        cand = (s1[..., :, None] + s2[..., None, :]).reshape(blk, PEER_HEADS, PEER_TOPK * PEER_TOPK)
        cidx = (i1[..., :, None] * N_KEYS + i2[..., None, :]).reshape(blk, PEER_HEADS, PEER_TOPK * PEER_TOPK)
        top_s, pos = lax.top_k(cand, PEER_TOPK)
        eidx = jnp.take_along_axis(cidx, pos, axis=-1)
        gate = jax.nn.softmax(top_s, axis=-1)
        ue = u[eidx]
        ve = v[eidx]
        act = jax.nn.gelu(jnp.einsum('td,thkd->thk', hb, ue, preferred_element_type=jnp.float32))
        return jnp.einsum('thk,thkd->td', (gate * act).astype(ve.dtype), ve,
                          preferred_element_type=jnp.float32)

    out = lax.map(block, (h, qall)).reshape(nb * blk, D_MODEL)
    return out.reshape(shape)


def _shared_kv_rows(x, kv_ln, kv_w, k_gain):
    B, L, _ = x.shape
    kv = _proj(x, kv_ln, kv_w).reshape(B, L, 6, NSA_KV_HEADS, NSA_DH)
    k_sel = _rmsnorm(kv[:, :, 2], k_gain[1])
    k_win = _rmsnorm(kv[:, :, 4], k_gain[2])
    rows = jnp.stack([kv[:, :, 0], kv[:, :, 1], k_sel, kv[:, :, 3]], axis=2)
    win = jnp.stack([k_win, kv[:, :, 5]], axis=2)
    return rows, win


def _compress(seq, w1, pe, w2):
    B, T, G, Dh = seq.shape
    n_chunk = T // CMP_STRIDE
    c = seq[:, :n_chunk * CMP_STRIDE].reshape(B, n_chunk, CMP_STRIDE, G, Dh)
    first = jnp.einsum('bcrgd,rdh->bcgh', c, w1[:CMP_STRIDE])
    second = jnp.einsum('bcrgd,rdh->bcgh', c, w1[CMP_STRIDE:])
    pe_term = jnp.einsum('rd,rdh->h', pe, w1)
    hid = jax.nn.gelu(first[:, :-1] + second[:, 1:] + pe_term)
    return jnp.einsum('bcgh,hd->bcgd', hid, w2)


def _nsa_context(rows, win_all, win_pos0, q_block, band_len, k_gain, phi_w1, phi_pe, phi_w2):
    B, T = rows.shape[:2]
    k_cmp = _rmsnorm(_compress(rows[:, :, 0], phi_w1[0], phi_pe[0], phi_w2[0]), k_gain[0])
    v_cmp = _compress(rows[:, :, 1], phi_w1[1], phi_pe[1], phi_w2[1])
    n_cmp = k_cmp.shape[1]
    cmp_start = jnp.arange(n_cmp, dtype=jnp.int32) * CMP_STRIDE
    cmp_end = cmp_start + CMP_BLOCK - 1
    n_sel = -(-T // SEL_BLOCK)
    sel = jnp.pad(rows[:, :, 2:], ((0, 0), (0, n_sel * SEL_BLOCK - T), (0, 0), (0, 0), (0, 0)))
    sel = sel.reshape(B, n_sel, SEL_BLOCK, 2, NSA_KV_HEADS, NSA_DH).transpose(3, 0, 4, 1, 2, 5)
    sel_start = jnp.arange(n_sel, dtype=jnp.int32) * SEL_BLOCK
    overlap = ((cmp_start[:, None] < sel_start[None, :] + SEL_BLOCK)
               & (cmp_end[:, None] >= sel_start[None, :])).astype(jnp.float32)
    return (k_cmp, v_cmp, cmp_end, sel[0], sel[1], overlap,
            win_all[:, :, 0], win_all[:, :, 1], win_pos0, q_block, band_len)


def _nsa_mixer(x, ln, pos0, ctx, w_qg, q_gain, w_out):
    k_cmp, v_cmp, cmp_end, k_sel, v_sel, overlap, k_win, v_win, win_pos0, q_block, band_len = ctx
    B, L, _ = x.shape
    G, HPG, DH = NSA_KV_HEADS, NSA_HPG, NSA_DH
    nb = L // q_block
    proj = _proj(x, ln, w_qg)
    q = _rmsnorm(proj[..., :NSA_HEADS * DH].reshape(B, L, G, HPG, DH), q_gain)
    gates = jax.nn.sigmoid(proj[..., NSA_HEADS * DH:]).reshape(B, L, 3, G, HPG)
    q_b = q.reshape(B, nb, q_block, G, HPG, DH).swapaxes(0, 1)
    g_b = gates.reshape(B, nb, q_block, 3, G, HPG).swapaxes(0, 1)
    t_b = (pos0 + jnp.arange(L, dtype=jnp.int32)).reshape(nb, q_block)
    starts = jnp.arange(nb, dtype=jnp.int32) * q_block
    slopes = _alibi_slopes()[None, None, :, :, None]
    n_sel = k_sel.shape[2]
    k_top = min(N_SEL, n_sel)
    blk_ids = jnp.arange(n_sel, dtype=jnp.int32)
    bi = jnp.arange(B)[:, None, None, None]
    gi = jnp.arange(G)[None, None, :, None]
    scale = NSA_DH ** -0.5

    def block(args):
        qq, gg, tt, st = args
        dist_c = (tt[:, None] - cmp_end[None, :]).astype(jnp.float32)
        s_c = jnp.einsum('bqgnd,bcgd->bqgnc', qq, k_cmp, preferred_element_type=jnp.float32) * scale
        s_c = s_c - slopes * dist_c[None, :, None, None, :]
        p_c = _masked_softmax(s_c, (dist_c >= 0)[None, :, None, None, :])
        o_c = jnp.einsum('bqgnc,bcgd->bqgnd', p_c, v_cmp, preferred_element_type=jnp.float32)
        imp = jnp.einsum('bqgnc,cj->bqgj', p_c, overlap)
        cur = tt // SEL_BLOCK
        forced = ((blk_ids[None] == 0) | (blk_ids[None] == cur[:, None])
                  | (blk_ids[None] == cur[:, None] - 1))
        visible = blk_ids[None] * SEL_BLOCK <= tt[:, None]
        score = jnp.where(forced[None, :, None], BIG, jnp.where(visible[None, :, None], imp, NEG))
        _, idx = lax.top_k(score, k_top)
        qb = tt.shape[0]
        ks = k_sel[bi, gi, idx].reshape(B, qb, G, k_top * SEL_BLOCK, DH)
        vs = v_sel[bi, gi, idx].reshape(B, qb, G, k_top * SEL_BLOCK, DH)
        spos = (idx[..., None] * SEL_BLOCK + jnp.arange(SEL_BLOCK, dtype=jnp.int32)).reshape(
            B, qb, G, k_top * SEL_BLOCK)
        dist_s = (tt[None, :, None, None] - spos).astype(jnp.float32)
        s_s = jnp.einsum('bqgnd,bqgkd->bqgnk', qq, ks, preferred_element_type=jnp.float32) * scale
        s_s = s_s - slopes * dist_s[:, :, :, None]
        p_s = _masked_softmax(s_s, (dist_s >= 0)[:, :, :, None])
        o_s = jnp.einsum('bqgnk,bqgkd->bqgnd', p_s, vs, preferred_element_type=jnp.float32)
        kw = lax.dynamic_slice_in_dim(k_win, st, band_len, axis=1)
        vw = lax.dynamic_slice_in_dim(v_win, st, band_len, axis=1)
        wpos = win_pos0 + st + jnp.arange(band_len, dtype=jnp.int32)
        dist_w = tt[:, None] - wpos[None, :]
        mask_w = (wpos[None, :] >= 0) & (dist_w >= 0) & (dist_w < WINDOW)
        s_w = jnp.einsum('bqgnd,blgd->bqgnl', qq, kw, preferred_element_type=jnp.float32) * scale
        s_w = s_w - slopes * dist_w.astype(jnp.float32)[None, :, None, None, :]
        p_w = _masked_softmax(s_w, mask_w[None, :, None, None, :])
        o_w = jnp.einsum('bqgnl,blgd->bqgnd', p_w, vw, preferred_element_type=jnp.float32)
        return (gg[:, :, 0, ..., None] * o_c + gg[:, :, 1, ..., None] * o_s
                + gg[:, :, 2, ..., None] * o_w)

    o = lax.map(block, (q_b, g_b, t_b, starts))
    o = o.swapaxes(0, 1).reshape(B, L, NSA_HEADS * DH)
    return _proj(o, None, w_out, normalize=False)


def kernel(x_prompt, x_sample, cache_kv, cache_win, state_gdn, state_conv, page_table,
           a_ln, a_w_in, a_conv_w, a_a_log, a_dt_bias, a_o_gain, a_w_out,
           kv_ln, kv_w, kv_k_gain, phi_w1, phi_pe, phi_w2,
           b_ln, b_w_qg, b_q_gain, b_w_out,
           p_ln, p_w_q, p_sub_keys, p_u, p_v):
    xp, xs = x_prompt, x_sample
    bp, lp = xp.shape[0], xp.shape[1]
    bs, ls = xs.shape[0], xs.shape[1]
    depth = p_ln.shape[0]
    n_a = a_ln.shape[0]
    past_len = page_table.shape[1] * cache_kv.shape[1]
    win_buf = cache_win.shape[1]
    q_block_p = Q_BLOCK if lp % Q_BLOCK == 0 else lp
    s_p, c_p, s_s, c_s = [], [], [], []
    for layer in range(depth):
        if layer < n_a:
            i = layer
            zs = jnp.zeros((bp, GDN_HEADS, GDN_DK, GDN_DV), jnp.float32)
            zc = jnp.zeros((bp, CONV_W - 1, GDN_CONV_DIM), xp.dtype)
            yp, sp, cp = _gdn_mixer(xp, a_ln[i], zs, zc, a_w_in[i], a_conv_w[i],
                                    a_a_log[i], a_dt_bias[i], a_o_gain[i], a_w_out[i])
            ys, ss, cs = _gdn_mixer(xs, a_ln[i], state_gdn[i], state_conv[i], a_w_in[i],
                                    a_conv_w[i], a_a_log[i], a_dt_bias[i], a_o_gain[i], a_w_out[i])
            xp = xp + yp
            xs = xs + ys
            s_p.append(sp)
            c_p.append(cp)
            s_s.append(ss)
            c_s.append(cs)
        else:
            j = layer - n_a
            xp = xp + _nsa_mixer(xp, b_ln[j], 0, ctx_p, b_w_qg[j], b_q_gain[j], b_w_out[j])
            xs = xs + _nsa_mixer(xs, b_ln[j], past_len, ctx_s, b_w_qg[j], b_q_gain[j], b_w_out[j])
        xp = xp + _peer(xp, p_ln[layer], p_w_q[layer], p_sub_keys[layer], p_u[layer], p_v[layer])
        xs = xs + _peer(xs, p_ln[layer], p_w_q[layer], p_sub_keys[layer], p_u[layer], p_v[layer])
        if layer == n_a - 1:
            rows_p, win_p = _shared_kv_rows(xp, kv_ln, kv_w, kv_k_gain)
            rows_s, win_s = _shared_kv_rows(xs, kv_ln, kv_w, kv_k_gain)
            win_pad_p = jnp.pad(win_p, ((0, 0), (WINDOW, 0), (0, 0), (0, 0), (0, 0)))
            ctx_p = _nsa_context(rows_p, win_pad_p, -WINDOW, q_block_p, WINDOW + q_block_p,
                                 kv_k_gain, phi_w1, phi_pe, phi_w2)
            past = cache_kv[page_table].reshape((bs, past_len) + cache_kv.shape[2:])
            full_s = jnp.concatenate([past, rows_s], axis=1)
            win_all_s = jnp.concatenate([cache_win, win_s], axis=1)
            ctx_s = _nsa_context(full_s, win_all_s, past_len - win_buf, ls, win_buf + ls,
                                 kv_k_gain, phi_w1, phi_pe, phi_w2)
            win_prompt = win_p[:, lp - min(WINDOW, lp):]
            win_sample = win_all_s[:, ls:]
    return (xp, xs, rows_p, win_prompt, jnp.stack(s_p), jnp.stack(c_p),
            rows_s, win_sample, jnp.stack(s_s), jnp.stack(c_s))
```

```python
import functools

import jax
import jax.numpy as jnp
from jax import lax
from jax.experimental import pallas as pl
from jax.experimental.pallas import tpu as pltpu

D_MODEL = 1024
GDN_HEADS = 8
GDN_DK = 128
GDN_DV = 128
GDN_QK_DIM = GDN_HEADS * GDN_DK
GDN_V_DIM = GDN_HEADS * GDN_DV
GDN_CONV_DIM = 2 * GDN_QK_DIM + GDN_V_DIM
CONV_W = 4
GDN_CHUNK = 64
NSA_HEADS = 16
NSA_KV_HEADS = 4
NSA_HPG = NSA_HEADS // NSA_KV_HEADS
NSA_DH = D_MODEL // NSA_HEADS
CMP_BLOCK = 32
CMP_STRIDE = 16
SEL_BLOCK = 64
N_SEL = 16
WINDOW = 512
Q_BLOCK = 128
PEER_HEADS = 8
PEER_DQ = 256
PEER_HALF = PEER_DQ // 2
N_KEYS = 128
PEER_TOPK = 16
PEER_TOKEN_BLOCK = 128
EPS = 1e-6
NEG = -1e30
BIG = 1e30

LANE = 128


def _norm_matmul_body(x_ref, g_ref, w_ref, o_ref, *, normalize):
    x = x_ref[...]
    if normalize:
        x = x * lax.rsqrt(jnp.mean(x * x, axis=-1, keepdims=True) + EPS) * g_ref[...]
    o_ref[...] = jnp.dot(x.astype(jnp.bfloat16), w_ref[...], preferred_element_type=jnp.float32)


def _norm_matmul(x, gain, w, *, normalize=True):
    m, k = x.shape
    n = w.shape[1]
    n_pad = -(-n // LANE) * LANE
    w = jnp.pad(w, ((0, 0), (0, n_pad - n))).astype(jnp.bfloat16)
    tm = 512 if m % 512 == 0 else m
    tn = next(t for t in (512, 384, 256, 128) if n_pad % t == 0)
    if gain is None:
        gain = jnp.ones((k,), jnp.float32)
    out = pl.pallas_call(
        functools.partial(_norm_matmul_body, normalize=normalize),
        grid=(m // tm, n_pad // tn),
        in_specs=[pl.BlockSpec((tm, k), lambda i, j: (i, 0)),
                  pl.BlockSpec((1, k), lambda i, j: (0, 0)),
                  pl.BlockSpec((k, tn), lambda i, j: (0, j))],
        out_specs=pl.BlockSpec((tm, tn), lambda i, j: (i, j)),
        out_shape=jax.ShapeDtypeStruct((m, n_pad), jnp.float32),
        compiler_params=pltpu.CompilerParams(dimension_semantics=("arbitrary", "arbitrary")),
        name="norm_matmul",
    )(x, gain.reshape(1, k).astype(jnp.float32), w)
    return out[:, :n]


def _proj(x, gain, w, normalize=True):
    b, l, d = x.shape
    return _norm_matmul(x.reshape(b * l, d), gain, w, normalize=normalize).reshape(b, l, -1)


def _rmsnorm(x, gain):
    xf = x.astype(jnp.float32)
    y = xf * lax.rsqrt(jnp.mean(xf * xf, axis=-1, keepdims=True) + EPS)
    return (y * gain.astype(jnp.float32)).astype(x.dtype)


def _l2norm(x):
    xf = x.astype(jnp.float32)
    return xf * lax.rsqrt(jnp.sum(xf * xf, axis=-1, keepdims=True) + EPS)


def _masked_softmax(s, mask):
    s = jnp.where(mask, s, NEG)
    m = jnp.max(s, axis=-1, keepdims=True)
    p = jnp.exp(s - m) * mask
    return p / jnp.maximum(jnp.sum(p, axis=-1, keepdims=True), 1e-30)


def _alibi_slopes():
    h = jnp.arange(1, NSA_HEADS + 1, dtype=jnp.float32)
    return jnp.exp2(-8.0 * h / NSA_HEADS).reshape(NSA_KV_HEADS, NSA_HPG)


def _short_conv(u, buf0, w):
    L = u.shape[1]
    buf = jnp.concatenate([buf0.astype(u.dtype), u], axis=1)
    y = sum(buf[:, i:i + L] * w[i] for i in range(CONV_W))
    return jax.nn.silu(y), buf[:, L:]


def _gated_delta_rule(q, k, v, g, beta, s0):
    Bsz, L, H, DK = q.shape
    C = GDN_CHUNK if L % GDN_CHUNK == 0 else L
    n = L // C

    def chunks(t):
        t = t.reshape((Bsz, n, C, H) + t.shape[3:])
        return jnp.moveaxis(t, (1, 3), (0, 2))

    qc, kc, vc, gc, bc = chunks(q), chunks(k), chunks(v), chunks(g), chunks(beta)
    gcum = jnp.cumsum(gc, axis=-1)
    idx = jnp.arange(C)
    lower = idx[:, None] >= idx[None, :]
    strict = idx[:, None] > idx[None, :]
    diff = gcum[..., :, None] - gcum[..., None, :]
    decay = jnp.where(lower, jnp.exp(jnp.where(lower, diff, 0.0)), 0.0)
    kb = kc * bc[..., None]
    a_mat = jnp.where(strict, jnp.einsum('nbhik,nbhjk->nbhij', kb, kc) * decay, 0.0)
    eye = jnp.eye(C, dtype=jnp.float32)
    t_mat = lax.linalg.triangular_solve(eye + a_mat, jnp.broadcast_to(eye, a_mat.shape),
                                        left_side=True, lower=True, unit_diagonal=True)
    u = t_mat @ (vc * bc[..., None])
    w = t_mat @ (kb * jnp.exp(gcum)[..., None])
    qk = jnp.where(lower, jnp.einsum('nbhik,nbhjk->nbhij', qc, kc) * decay, 0.0)

    def step(S, xs):
        q_i, k_i, u_i, w_i, g_i, qk_i = xs
        v_new = u_i - w_i @ S
        o = (q_i * jnp.exp(g_i)[..., None]) @ S + qk_i @ v_new
        g_last = g_i[..., -1]
        k_dec = k_i * jnp.exp(g_last[..., None] - g_i)[..., None]
        S = S * jnp.exp(g_last)[..., None, None] + jnp.einsum('bhck,bhcv->bhkv', k_dec, v_new)
        return S, o

    S, o = lax.scan(step, s0, (qc, kc, u, w, gcum, qk))
    o = jnp.moveaxis(o, (0, 2), (1, 3)).reshape(Bsz, L, H, v.shape[-1])
    return o, S


def _gdn_mixer(x, ln, s0, conv0, w_in, conv_w, a_log, dt_bias, o_gain, w_out):
    B, L, _ = x.shape
    proj = _proj(x, ln, w_in)
    qkv, conv_state = _short_conv(proj[..., :GDN_CONV_DIM], conv0, conv_w)
    z = proj[..., GDN_CONV_DIM:GDN_CONV_DIM + GDN_V_DIM].reshape(B, L, GDN_HEADS, GDN_DV)
    a = proj[..., GDN_CONV_DIM + GDN_V_DIM:GDN_CONV_DIM + GDN_V_DIM + GDN_HEADS]
    b = proj[..., GDN_CONV_DIM + GDN_V_DIM + GDN_HEADS:]
    q = _l2norm(qkv[..., :GDN_QK_DIM].reshape(B, L, GDN_HEADS, GDN_DK)) * (GDN_DK ** -0.5)
    k = _l2norm(qkv[..., GDN_QK_DIM:2 * GDN_QK_DIM].reshape(B, L, GDN_HEADS, GDN_DK))
    v = qkv[..., 2 * GDN_QK_DIM:].reshape(B, L, GDN_HEADS, GDN_DV)
    g = -jnp.exp(a_log) * jax.nn.softplus(a + dt_bias)
    beta = jax.nn.sigmoid(b)
    o, s_new = _gated_delta_rule(q, k, v, g, beta, s0)
    o = _rmsnorm(o, o_gain) * jax.nn.silu(z)
    return _proj(o.reshape(B, L, GDN_V_DIM), None, w_out, normalize=False), s_new, conv_state


PEER_E_TILE = 1024
PEER_T_TILE = 512
PEER_VMEM_BYTES = 52 * 1024 * 1024


def _peer_dense_body(hb_ref, x_ref, u_ref, vt_ref, e1_ref, n_ref, r2_ref, e2_ref, o_ref,
                     act_ref, p_ref, acc_ref, *, t_tile):
    j = pl.program_id(1)

    @pl.when(j == 0)
    def _():
        acc_ref[...] = jnp.zeros_like(acc_ref)

    act_ref[...] = lax.dot_general(u_ref[...], hb_ref[...], (((1,), (1,)), ((), ())),
                                   preferred_element_type=jnp.float32)
    a_per_step = PEER_E_TILE // N_KEYS

    def per_key(a_l, carry):
        row0 = pl.multiple_of(a_l * N_KEYS, N_KEYS)
        for tc in range(t_tile // LANE):
            ts = slice(tc * LANE, (tc + 1) * LANE)
            w = jnp.zeros((N_KEYS, LANE), jnp.float32)
            for h in range(PEER_HEADS):
                n_row = n_ref[h, a_l, :, ts]
                e1_row = e1_ref[h, a_l, :, ts]
                w = w + jnp.where(r2_ref[h, :, ts] < n_row, e2_ref[h, :, ts], 0.0) * e1_row
            act = act_ref[pl.ds(row0, N_KEYS), ts]
            p_ref[pl.ds(row0, N_KEYS), ts] = (w * jax.nn.gelu(act)).astype(jnp.bfloat16)
        return carry

    lax.fori_loop(0, a_per_step, per_key, 0)
    acc_ref[...] += jnp.dot(vt_ref[...], p_ref[...], preferred_element_type=jnp.float32)

    @pl.when(j == pl.num_programs(1) - 1)
    def _():
        o_ref[...] = x_ref[...] + acc_ref[...].T


def _peer_dense(x, hb, u_bf, vt_bf, e1, n, r2, e2):
    t = x.shape[0]
    n_exp = u_bf.shape[0]
    t_tile = PEER_T_TILE if t % PEER_T_TILE == 0 else t
    sel_spec = pl.BlockSpec((PEER_HEADS, N_KEYS, t_tile), lambda i, j: (0, 0, i))
    a_per_step = PEER_E_TILE // N_KEYS
    row_spec = pl.BlockSpec((PEER_HEADS, a_per_step, 1, t_tile), lambda i, j: (0, j, 0, i))
    e1 = e1.reshape(PEER_HEADS, N_KEYS, 1, t)
    n = n.reshape(PEER_HEADS, N_KEYS, 1, t)
    return pl.pallas_call(
        functools.partial(_peer_dense_body, t_tile=t_tile),
        grid=(t // t_tile, n_exp // PEER_E_TILE),
        in_specs=[pl.BlockSpec((t_tile, D_MODEL), lambda i, j: (i, 0)),
                  pl.BlockSpec((t_tile, D_MODEL), lambda i, j: (i, 0)),
                  pl.BlockSpec((PEER_E_TILE, D_MODEL), lambda i, j: (j, 0)),
                  pl.BlockSpec((D_MODEL, PEER_E_TILE), lambda i, j: (0, j)),
                  row_spec, row_spec, sel_spec, sel_spec],
        out_specs=pl.BlockSpec((t_tile, D_MODEL), lambda i, j: (i, 0)),
        out_shape=jax.ShapeDtypeStruct((t, D_MODEL), jnp.float32),
        scratch_shapes=[pltpu.VMEM((PEER_E_TILE, t_tile), jnp.float32),
                        pltpu.VMEM((PEER_E_TILE, t_tile), jnp.bfloat16),
                        pltpu.VMEM((D_MODEL, t_tile), jnp.float32)],
        compiler_params=pltpu.CompilerParams(dimension_semantics=("arbitrary", "arbitrary"),
                                             vmem_limit_bytes=PEER_VMEM_BYTES),
        name="peer_dense",
    )(hb, x, u_bf, vt_bf, e1, n, r2, e2)


def _peer_select_tables(q, sub_keys):
    s = jnp.einsum('thpd,hpkd->thpk', q.astype(jnp.bfloat16), sub_keys.astype(jnp.bfloat16),
                   preferred_element_type=jnp.float32)
    s1, i1 = lax.top_k(s[:, :, 0], PEER_TOPK)
    s2, i2 = lax.top_k(s[:, :, 1], PEER_TOPK)
    cand = (s1[..., :, None] + s2[..., None, :]).reshape(s.shape[0], PEER_HEADS, PEER_TOPK * PEER_TOPK)
    top_s, pos = lax.top_k(cand, PEER_TOPK)
    j1 = pos // PEER_TOPK
    ranks = jnp.arange(PEER_TOPK, dtype=jnp.int32)
    n_rank = jnp.sum((j1[..., None] == ranks).astype(jnp.float32), axis=2)
    z = jnp.sum(jnp.exp(top_s - top_s[..., :1]), axis=-1, keepdims=True)
    e1s = jnp.exp(s1 - s1[..., :1]) / z
    e2s = jnp.exp(s2 - s2[..., :1])
    keys = jnp.arange(N_KEYS, dtype=jnp.int32)

    def to_keys(idx, vals, fill):
        tab = jnp.sum(jnp.where(idx[..., None] == keys, (vals - fill)[..., None], 0.0), axis=2) + fill
        return tab.transpose(1, 2, 0)

    rank_f = jnp.broadcast_to(ranks.astype(jnp.float32), i2.shape)
    return (to_keys(i1, e1s, 0.0), to_keys(i1, n_rank, 0.0),
            to_keys(i2, rank_f, float(PEER_TOPK)), to_keys(i2, e2s, 0.0))


def _peer(x, ln, w_q, sub_keys, u, v):
    shape = x.shape
    xf = x.reshape(-1, D_MODEL)
    hb = _rmsnorm(xf, ln).astype(jnp.bfloat16)
    q = _norm_matmul(xf, ln, w_q).reshape(-1, PEER_HEADS, 2, PEER_HALF)
    e1, n, r2, e2 = _peer_select_tables(q, sub_keys)
    out = _peer_dense(xf, hb, u.astype(jnp.bfloat16), v.T.astype(jnp.bfloat16), e1, n, r2, e2)
    return out.reshape(shape)


def _shared_kv_rows(x, kv_ln, kv_w, k_gain):
    B, L, _ = x.shape
    kv = _proj(x, kv_ln, kv_w).reshape(B, L, 6, NSA_KV_HEADS, NSA_DH)
    k_sel = _rmsnorm(kv[:, :, 2], k_gain[1])
    k_win = _rmsnorm(kv[:, :, 4], k_gain[2])
    rows = jnp.stack([kv[:, :, 0], kv[:, :, 1], k_sel, kv[:, :, 3]], axis=2)
    win = jnp.stack([k_win, kv[:, :, 5]], axis=2)
    return rows, win


def _compress(seq, w1, pe, w2):
    B, T, G, Dh = seq.shape
    n_chunk = T // CMP_STRIDE
    c = seq[:, :n_chunk * CMP_STRIDE].reshape(B, n_chunk, CMP_STRIDE, G, Dh)
    first = jnp.einsum('bcrgd,rdh->bcgh', c, w1[:CMP_STRIDE])
    second = jnp.einsum('bcrgd,rdh->bcgh', c, w1[CMP_STRIDE:])
    pe_term = jnp.einsum('rd,rdh->h', pe, w1)
    hid = jax.nn.gelu(first[:, :-1] + second[:, 1:] + pe_term)
    return jnp.einsum('bcgh,hd->bcgd', hid, w2)


def _nsa_context(rows, win_all, win_pos0, q_block, band_len, k_gain, phi_w1, phi_pe, phi_w2):
    B, T = rows.shape[:2]
    k_cmp = _rmsnorm(_compress(rows[:, :, 0], phi_w1[0], phi_pe[0], phi_w2[0]), k_gain[0])
    v_cmp = _compress(rows[:, :, 1], phi_w1[1], phi_pe[1], phi_w2[1])
    n_cmp = k_cmp.shape[1]
    cmp_start = jnp.arange(n_cmp, dtype=jnp.int32) * CMP_STRIDE
    cmp_end = cmp_start + CMP_BLOCK - 1
    n_sel = -(-T // SEL_BLOCK)
    sel = jnp.pad(rows[:, :, 2:], ((0, 0), (0, n_sel * SEL_BLOCK - T), (0, 0), (0, 0), (0, 0)))
    sel = sel.reshape(B, n_sel, SEL_BLOCK, 2, NSA_KV_HEADS, NSA_DH).transpose(3, 0, 4, 1, 2, 5)
    sel_start = jnp.arange(n_sel, dtype=jnp.int32) * SEL_BLOCK
    overlap = ((cmp_start[:, None] < sel_start[None, :] + SEL_BLOCK)
               & (cmp_end[:, None] >= sel_start[None, :])).astype(jnp.float32)
    return (k_cmp, v_cmp, cmp_end, sel[0], sel[1], overlap,
            win_all[:, :, 0], win_all[:, :, 1], win_pos0, q_block, band_len)


def _nsa_mixer(x, ln, pos0, ctx, w_qg, q_gain, w_out):
    k_cmp, v_cmp, cmp_end, k_sel, v_sel, overlap, k_win, v_win, win_pos0, q_block, band_len = ctx
    B, L, _ = x.shape
    G, HPG, DH = NSA_KV_HEADS, NSA_HPG, NSA_DH
    nb = L // q_block
    proj = _proj(x, ln, w_qg)
    q = _rmsnorm(proj[..., :NSA_HEADS * DH].reshape(B, L, G, HPG, DH), q_gain)
    gates = jax.nn.sigmoid(proj[..., NSA_HEADS * DH:]).reshape(B, L, 3, G, HPG)
    q_b = q.reshape(B, nb, q_block, G, HPG, DH).swapaxes(0, 1)
    g_b = gates.reshape(B, nb, q_block, 3, G, HPG).swapaxes(0, 1)
    t_b = (pos0 + jnp.arange(L, dtype=jnp.int32)).reshape(nb, q_block)
    starts = jnp.arange(nb, dtype=jnp.int32) * q_block
    slopes = _alibi_slopes()[None, None, :, :, None]
    n_sel = k_sel.shape[2]
    k_top = min(N_SEL, n_sel)
    blk_ids = jnp.arange(n_sel, dtype=jnp.int32)
    bi = jnp.arange(B)[:, None, None, None]
    gi = jnp.arange(G)[None, None, :, None]
    scale = NSA_DH ** -0.5

    def block(args):
        qq, gg, tt, st = args
        dist_c = (tt[:, None] - cmp_end[None, :]).astype(jnp.float32)
        s_c = jnp.einsum('bqgnd,bcgd->bqgnc', qq, k_cmp, preferred_element_type=jnp.float32) * scale
        s_c = s_c - slopes * dist_c[None, :, None, None, :]
        p_c = _masked_softmax(s_c, (dist_c >= 0)[None, :, None, None, :])
        o_c = jnp.einsum('bqgnc,bcgd->bqgnd', p_c, v_cmp, preferred_element_type=jnp.float32)
        imp = jnp.einsum('bqgnc,cj->bqgj', p_c, overlap)
        cur = tt // SEL_BLOCK
        forced = ((blk_ids[None] == 0) | (blk_ids[None] == cur[:, None])
                  | (blk_ids[None] == cur[:, None] - 1))
        visible = blk_ids[None] * SEL_BLOCK <= tt[:, None]
        score = jnp.where(forced[None, :, None], BIG, jnp.where(visible[None, :, None], imp, NEG))
        _, idx = lax.top_k(score, k_top)
        qb = tt.shape[0]
        ks = k_sel[bi, gi, idx].reshape(B, qb, G, k_top * SEL_BLOCK, DH)
        vs = v_sel[bi, gi, idx].reshape(B, qb, G, k_top * SEL_BLOCK, DH)
        spos = (idx[..., None] * SEL_BLOCK + jnp.arange(SEL_BLOCK, dtype=jnp.int32)).reshape(
            B, qb, G, k_top * SEL_BLOCK)
        dist_s = (tt[None, :, None, None] - spos).astype(jnp.float32)
        s_s = jnp.einsum('bqgnd,bqgkd->bqgnk', qq, ks, preferred_element_type=jnp.float32) * scale
        s_s = s_s - slopes * dist_s[:, :, :, None]
        p_s = _masked_softmax(s_s, (dist_s >= 0)[:, :, :, None])
        o_s = jnp.einsum('bqgnk,bqgkd->bqgnd', p_s, vs, preferred_element_type=jnp.float32)
        kw = lax.dynamic_slice_in_dim(k_win, st, band_len, axis=1)
        vw = lax.dynamic_slice_in_dim(v_win, st, band_len, axis=1)
        wpos = win_pos0 + st + jnp.arange(band_len, dtype=jnp.int32)
        dist_w = tt[:, None] - wpos[None, :]
        mask_w = (wpos[None, :] >= 0) & (dist_w >= 0) & (dist_w < WINDOW)
        s_w = jnp.einsum('bqgnd,blgd->bqgnl', qq, kw, preferred_element_type=jnp.float32) * scale
        s_w = s_w - slopes * dist_w.astype(jnp.float32)[None, :, None, None, :]
        p_w = _masked_softmax(s_w, mask_w[None, :, None, None, :])
        o_w = jnp.einsum('bqgnl,blgd->bqgnd', p_w, vw, preferred_element_type=jnp.float32)
        return (gg[:, :, 0, ..., None] * o_c + gg[:, :, 1, ..., None] * o_s
                + gg[:, :, 2, ..., None] * o_w)

    o = lax.map(block, (q_b, g_b, t_b, starts))
    o = o.swapaxes(0, 1).reshape(B, L, NSA_HEADS * DH)
    return _proj(o, None, w_out, normalize=False)


def kernel(x_prompt, x_sample, cache_kv, cache_win, state_gdn, state_conv, page_table,
           a_ln, a_w_in, a_conv_w, a_a_log, a_dt_bias, a_o_gain, a_w_out,
           kv_ln, kv_w, kv_k_gain, phi_w1, phi_pe, phi_w2,
           b_ln, b_w_qg, b_q_gain, b_w_out,
           p_ln, p_w_q, p_sub_keys, p_u, p_v):
    xp, xs = x_prompt, x_sample
    bp, lp = xp.shape[0], xp.shape[1]
    bs, ls = xs.shape[0], xs.shape[1]
    depth = p_ln.shape[0]
    n_a = a_ln.shape[0]
    past_len = page_table.shape[1] * cache_kv.shape[1]
    win_buf = cache_win.shape[1]
    q_block_p = Q_BLOCK if lp % Q_BLOCK == 0 else lp
    s_p, c_p, s_s, c_s = [], [], [], []
    for layer in range(depth):
        if layer < n_a:
            i = layer
            zs = jnp.zeros((bp, GDN_HEADS, GDN_DK, GDN_DV), jnp.float32)
            zc = jnp.zeros((bp, CONV_W - 1, GDN_CONV_DIM), xp.dtype)
            yp, sp, cp = _gdn_mixer(xp, a_ln[i], zs, zc, a_w_in[i], a_conv_w[i],
                                    a_a_log[i], a_dt_bias[i], a_o_gain[i], a_w_out[i])
            ys, ss, cs = _gdn_mixer(xs, a_ln[i], state_gdn[i], state_conv[i], a_w_in[i],
                                    a_conv_w[i], a_a_log[i], a_dt_bias[i], a_o_gain[i], a_w_out[i])
            xp = xp + yp
            xs = xs + ys
            s_p.append(sp)
            c_p.append(cp)
            s_s.append(ss)
            c_s.append(cs)
        else:
            j = layer - n_a
            xp = xp + _nsa_mixer(xp, b_ln[j], 0, ctx_p, b_w_qg[j], b_q_gain[j], b_w_out[j])
            xs = xs + _nsa_mixer(xs, b_ln[j], past_len, ctx_s, b_w_qg[j], b_q_gain[j], b_w_out[j])
        xp = _peer(xp, p_ln[layer], p_w_q[layer], p_sub_keys[layer], p_u[layer], p_v[layer])
        xs = _peer(xs, p_ln[layer], p_w_q[layer], p_sub_keys[layer], p_u[layer], p_v[layer])
        if layer == n_a - 1:
            rows_p, win_p = _shared_kv_rows(xp, kv_ln, kv_w, kv_k_gain)
            rows_s, win_s = _shared_kv_rows(xs, kv_ln, kv_w, kv_k_gain)
            win_pad_p = jnp.pad(win_p, ((0, 0), (WINDOW, 0), (0, 0), (0, 0), (0, 0)))
            ctx_p = _nsa_context(rows_p, win_pad_p, -WINDOW, q_block_p, WINDOW + q_block_p,
                                 kv_k_gain, phi_w1, phi_pe, phi_w2)
            past = cache_kv[page_table].reshape((bs, past_len) + cache_kv.shape[2:])
            full_s = jnp.concatenate([past, rows_s], axis=1)
            win_all_s = jnp.concatenate([cache_win, win_s], axis=1)
            ctx_s = _nsa_context(full_s, win_all_s, past_len - win_buf, ls, win_buf + ls,
                                 kv_k_gain, phi_w1, phi_pe, phi_w2)
            win_prompt = win_p[:, lp - min(WINDOW, lp):]
            win_sample = win_all_s[:, ls:]
    return (xp, xs, rows_p, win_prompt, jnp.stack(s_p), jnp.stack(c_p),
            rows_s, win_sample, jnp.stack(s_s), jnp.stack(c_s))
```
